```python
import jax
import jax.numpy as jnp
from jax import lax
import numpy as np

D_MODEL = 1024
BATCH = 8
SEQ = 2048
DEPTH = 2
DEC_BATCH = 128
DEC_SEQ = 8
PAST_LEN = 2048
PAGE_SIZE = 128

HEAD_DIM = 64
N_ATTN_HEADS = 8
D_ATTN = N_ATTN_HEADS * HEAD_DIM
D_CONV = D_MODEL - D_ATTN
D_MIX = D_ATTN + D_CONV
D_IN = 3 * D_ATTN + 2 * D_CONV
CONV_WIDTH = 31
CONV_HIST = CONV_WIDTH - 1
DIL_PAIRS = ((128, 1), (512, 4), (2048, 16))
WINDOW_MAX = max(w for w, _ in DIL_PAIRS)
QBLOCK = 128
ROT_DIM = HEAD_DIM // 4
ROPE_THETA = 500000.0
D_FF = -(-8 * D_MODEL // (3 * 256)) * 256
DN_ALPHA = (2 * DEPTH) ** 0.25
DN_BETA = (8 * DEPTH) ** -0.25
LN_EPS = 1e-5
NEG_INF = -1e30

kernel_name = 'hymba_dilated_conformer_decoder_step'


def _layernorm(x, g, b):
    xf = x.astype(jnp.float32)
    mu = jnp.mean(xf, axis=-1, keepdims=True)
    var = jnp.mean(jnp.square(xf - mu), axis=-1, keepdims=True)
    y = (xf - mu) * lax.rsqrt(var + LN_EPS)
    return (y * g.astype(jnp.float32) + b.astype(jnp.float32)).astype(x.dtype)


def _rmsnorm(x, g):
    xf = x.astype(jnp.float32)
    y = xf * lax.rsqrt(jnp.mean(jnp.square(xf), axis=-1, keepdims=True) + LN_EPS)
    return (y * g.astype(jnp.float32)).astype(x.dtype)


def _rope(x, pos):
    half = ROT_DIM // 2
    inv_freq = jnp.power(ROPE_THETA, -jnp.arange(half, dtype=jnp.float32) * 2.0 / ROT_DIM)
    ang = pos.astype(jnp.float32)[:, None] * inv_freq[None, :]
    cos = jnp.cos(ang)[None, :, None, :]
    sin = jnp.sin(ang)[None, :, None, :]
    xf = x.astype(jnp.float32)
    x1 = xf[..., :half]
    x2 = xf[..., half:ROT_DIM]
    out = jnp.concatenate([x1 * cos - x2 * sin, x2 * cos + x1 * sin, xf[..., ROT_DIM:]], axis=-1)
    return out.astype(x.dtype)


def _band_attention(q, k, v, w_sub):
    n, l, h, dh = q.shape
    nb = -(-l // QBLOCK)
    lp = nb * QBLOCK
    pad = lp - l
    qb = jnp.pad(q, ((0, 0), (0, pad), (0, 0), (0, 0))).reshape(n, nb, QBLOCK, h, dh)

    def key_blocks(t):
        tp = jnp.pad(t, ((0, 0), (QBLOCK, pad), (0, 0), (0, 0))).reshape(n, nb + 1, QBLOCK, h, dh)
        return jnp.concatenate([tp[:, :-1], tp[:, 1:]], axis=2)

    kb = key_blocks(k)
    vb = key_blocks(v)
    s = jnp.einsum('nbqhd,nbkhd->nbhqk', qb, kb, preferred_element_type=jnp.float32)
    blk = jnp.arange(nb)[:, None, None] * QBLOCK
    qpos = blk + jnp.arange(QBLOCK)[None, :, None]
    kpos = blk - QBLOCK + jnp.arange(2 * QBLOCK)[None, None, :]
    mask = (kpos >= 0) & (kpos <= qpos) & (kpos >= qpos - w_sub)
    s = jnp.where(mask[None, :, None], s, NEG_INF)
    lse = jax.nn.logsumexp(s, axis=-1)
    p = jnp.exp(s - lse[..., None])
    o = jnp.einsum('nbhqk,nbkhd->nbqhd', p.astype(vb.dtype), vb, preferred_element_type=jnp.float32)
    o = o.reshape(n, lp, h, dh)[:, :l]
    lse = lse.transpose(0, 1, 3, 2).reshape(n, lp, h)[:, :l]
    return o, lse


def _merge_dilations(outs, lses):
    wts = jax.nn.softmax(jnp.stack(lses, axis=0), axis=0)
    return jnp.sum(wts[..., None] * jnp.stack(outs, axis=0), axis=0)


def _by_residue(t, dil):
    b, s, h, dh = t.shape
    return t.reshape(b, s // dil, dil, h, dh).transpose(0, 2, 1, 3, 4).reshape(b * dil, s // dil, h, dh)


def _dilated_attention_prompt(q, k, v):
    b, s, h, dh = q.shape
    outs, lses = [], []
    for window, dil in DIL_PAIRS:
        l = s // dil
        o, lse = _band_attention(_by_residue(q, dil), _by_residue(k, dil), _by_residue(v, dil), window // dil)
        outs.append(o.reshape(b, dil, l, h, dh).transpose(0, 2, 1, 3, 4).reshape(b, s, h, dh))
        lses.append(lse.reshape(b, dil, l, h).transpose(0, 2, 1, 3).reshape(b, s, h))
    return _merge_dilations(outs, lses)


def _dilated_attention_sample(q, k_all, v_all):
    b, t, h, dh = q.shape
    buf = k_all.shape[1] - t
    outs, lses = [], []
    for window, dil in DIL_PAIRS:
        n_keys = window // dil + 1
        idx = buf + jnp.arange(t)[:, None] - dil * jnp.arange(n_keys)[None, :]
        valid = idx >= 0
        idx = jnp.maximum(idx, 0)
        kg = k_all[:, idx]
        vg = v_all[:, idx]
        s = jnp.einsum('bthd,btjhd->bthj', q, kg, preferred_element_type=jnp.float32)
        s = jnp.where(valid[None, :, None, :], s, NEG_INF)
        lse = jax.nn.logsumexp(s, axis=-1)
        p = jnp.exp(s - lse[..., None])
        outs.append(jnp.einsum('bthj,btjhd->bthd', p.astype(vg.dtype), vg, preferred_element_type=jnp.float32))
        lses.append(lse)
    return _merge_dilations(outs, lses)


def _dwconv_causal(u_hist, conv_w, conv_b):
    y = lax.conv_general_dilated(u_hist, conv_w[:, None, :], window_strides=(1,), padding='VALID',
                                 dimension_numbers=('NWC', 'WIO', 'NWC'), feature_group_count=D_CONV)
    return y + conv_b


def _mixer(h, pos, w_in, conv_w, conv_b, conv_ln_g, conv_ln_b, beta_attn, beta_conv, w_out, kv_past, conv_past):
    b, l, _ = h.shape
    z = h @ w_in
    q, k, v, g = jnp.split(z, [D_ATTN, 2 * D_ATTN, 3 * D_ATTN], axis=-1)
    q = _rope(q.reshape(b, l, N_ATTN_HEADS, HEAD_DIM), pos) * (HEAD_DIM ** -0.5)
    k = _rope(k.reshape(b, l, N_ATTN_HEADS, HEAD_DIM), pos)
    v = v.reshape(b, l, N_ATTN_HEADS, HEAD_DIM)
    if kv_past is None:
        o = _dilated_attention_prompt(q, k, v)
    else:
        k_all = jnp.concatenate([kv_past[0], k], axis=1)
        v_all = jnp.concatenate([kv_past[1], v], axis=1)
        o = _dilated_attention_sample(q, k_all, v_all)
    o = o.astype(h.dtype).reshape(b, l, D_ATTN)
    ga, gb = jnp.split(g, 2, axis=-1)
    u = ga * jax.nn.sigmoid(gb)
    if conv_past is None:
        u_hist = jnp.pad(u, ((0, 0), (CONV_HIST, 0), (0, 0)))
    else:
        u_hist = jnp.concatenate([conv_past, u], axis=1)
    cz = jax.nn.silu(_layernorm(_dwconv_causal(u_hist, conv_w, conv_b), conv_ln_g, conv_ln_b))
    merged = jnp.concatenate([_rmsnorm(o, beta_attn), _rmsnorm(cz, beta_conv)], axis=-1) @ w_out
    return merged, k, v, u_hist[:, -CONV_HIST:]


def _forward(x, c, pos, past_k, past_v, past_conv, params):
    (w_in, conv_w, conv_b, conv_ln_g, conv_ln_b, beta_attn, beta_conv, w_out,
     ln1_g, ln1_b, w_gate, w_up, w_down, ln2_g, ln2_b, w_ada, b_ada) = params
    new_k, new_v, new_conv = [], [], []
    for i in range(DEPTH):
        mod = jax.nn.silu(c) @ w_ada[i] + b_ada[i]
        sh1, sc1, g1, sh2, sc2, g2 = jnp.split(mod[:, None, :], 6, axis=-1)
        h = x * (1 + sc1) + sh1
        if past_k is None:
            kv_past, conv_past = None, None
        else:
            kv_past, conv_past = (past_k[i], past_v[i]), past_conv[i]
        mix, k, v, conv_state = _mixer(h, pos, w_in[i], conv_w[i], conv_b[i], conv_ln_g[i], conv_ln_b[i],
                                       beta_attn[i], beta_conv[i], w_out[i], kv_past, conv_past)
        x = _layernorm(DN_ALPHA * x + g1 * mix, ln1_g[i], ln1_b[i])
        h = x * (1 + sc2) + sh2
        f = (jax.nn.silu(h @ w_gate[i]) * (h @ w_up[i])) @ w_down[i]
        x = _layernorm(DN_ALPHA * x + g2 * f, ln2_g[i], ln2_b[i])
        if past_k is None:
            keep = min(WINDOW_MAX, x.shape[1])
            k = k[:, -keep:]
            v = v[:, -keep:]
        new_k.append(k)
        new_v.append(v)
        new_conv.append(conv_state)
    return x, jnp.stack(new_k), jnp.stack(new_v), jnp.stack(new_conv)


def setup_inputs(seed: int = 0) -> dict:
    key = jax.random.key(seed)
    ks = jax.random.split(key, 26)
    f32 = jnp.float32

    def nrm(k, shape, scale):
        return jax.random.normal(k, shape, f32) * scale

    buf = min(WINDOW_MAX, PAST_LEN)
    col_scale = jnp.ones((D_IN,), f32).at[2 * D_ATTN:3 * D_ATTN].set(DN_BETA)
    return {
        'x_prompt': nrm(ks[0], (BATCH, SEQ, D_MODEL), 1.0),
        'x_sample': nrm(ks[1], (DEC_BATCH, DEC_SEQ, D_MODEL), 1.0),
        'cache_k': nrm(ks[2], (DEPTH, DEC_BATCH, buf, N_ATTN_HEADS, HEAD_DIM), 1.0),
        'cache_v': nrm(ks[3], (DEPTH, DEC_BATCH, buf, N_ATTN_HEADS, HEAD_DIM), 1.0),
        'state_conv': nrm(ks[4], (DEPTH, DEC_BATCH, CONV_HIST, D_CONV), 0.5),
        'c_prompt': nrm(ks[5], (BATCH, D_MODEL), 1.0),
        'c_sample': nrm(ks[6], (DEC_BATCH, D_MODEL), 1.0),
        'w_in': nrm(ks[7], (DEPTH, D_MODEL, D_IN), D_MODEL ** -0.5) * col_scale,
        'conv_w': nrm(ks[8], (DEPTH, CONV_WIDTH, D_CONV), CONV_WIDTH ** -0.5),
        'conv_b': nrm(ks[9], (DEPTH, D_CONV), 0.02),
        'conv_ln_g': 1.0 + nrm(ks[10], (DEPTH, D_CONV), 0.05),
        'conv_ln_b': nrm(ks[11], (DEPTH, D_CONV), 0.02),
        'beta_attn': 1.0 + nrm(ks[12], (DEPTH, D_ATTN), 0.05),
        'beta_conv': 1.0 + nrm(ks[13], (DEPTH, D_CONV), 0.05),
        'w_out': nrm(ks[14], (DEPTH, D_MIX, D_MODEL), D_MIX ** -0.5 * DN_BETA),
        'ln1_g': 1.0 + nrm(ks[15], (DEPTH, D_MODEL), 0.05),
        'ln1_b': nrm(ks[16], (DEPTH, D_MODEL), 0.02),
        'w_gate': nrm(ks[17], (DEPTH, D_MODEL, D_FF), D_MODEL ** -0.5),
        'w_up': nrm(ks[18], (DEPTH, D_MODEL, D_FF), D_MODEL ** -0.5),
        'w_down': nrm(ks[19], (DEPTH, D_FF, D_MODEL), D_FF ** -0.5 * DN_BETA),
        'ln2_g': 1.0 + nrm(ks[20], (DEPTH, D_MODEL), 0.05),
        'ln2_b': nrm(ks[21], (DEPTH, D_MODEL), 0.02),
        'w_ada': nrm(ks[22], (DEPTH, D_MODEL, 6 * D_MODEL), 0.5 * D_MODEL ** -0.5),
        'b_ada': nrm(ks[23], (DEPTH, 6 * D_MODEL), 0.02),
    }


def reference(x_prompt, x_sample, cache_k, cache_v, state_conv, c_prompt, c_sample, w_in, conv_w, conv_b,
              conv_ln_g, conv_ln_b, beta_attn, beta_conv, w_out, ln1_g, ln1_b, w_gate, w_up, w_down,
              ln2_g, ln2_b, w_ada, b_ada):
    params = (w_in, conv_w, conv_b, conv_ln_g, conv_ln_b, beta_attn, beta_conv, w_out,
              ln1_g, ln1_b, w_gate, w_up, w_down, ln2_g, ln2_b, w_ada, b_ada)
    pos_prompt = jnp.arange(x_prompt.shape[1])
    pos_sample = PAST_LEN + jnp.arange(x_sample.shape[1])
    y_prompt, k_p, v_p, conv_p = _forward(x_prompt, c_prompt, pos_prompt, None, None, None, params)
    y_sample, k_s, v_s, conv_s = _forward(x_sample, c_sample, pos_sample, cache_k, cache_v, state_conv, params)
    return (y_prompt, y_sample, k_p, v_p, conv_p, k_s, v_s, conv_s)
```

```python
import functools
import math

import numpy as np
import jax
import jax.numpy as jnp
from jax import lax
from jax.experimental import pallas as pl
from jax.experimental.pallas import tpu as pltpu

D_MODEL = 1024
HEAD_DIM = 64
N_HEADS = 8
D_ATTN = N_HEADS * HEAD_DIM
D_CONV = D_MODEL - D_ATTN
D_IN = 3 * D_ATTN + 2 * D_CONV
CONV_WIDTH = 31
CONV_HIST = CONV_WIDTH - 1
DIL_PAIRS = ((128, 1), (512, 4), (2048, 16))
WINDOW_MAX = max(w for w, _ in DIL_PAIRS)
ROT_DIM = HEAD_DIM // 4
ROPE_THETA = 500000.0
D_FF = 2816
PAST_LEN = 2048
LN_EPS = 1e-5
NEG_INF = -1e30

V7X_LANES = 128
V7X_SUBLANES = 8
V7X_VMEM_LIMIT_BYTES = 56 * 1024 * 1024

HEAD_PAIRS = D_ATTN // V7X_LANES
HIST_ROWS = 32
HIST_PAD = HIST_ROWS - CONV_HIST
ATT_BLOCK = 256
FFN_CHUNK = D_FF // 2

F32 = jnp.float32
BF16 = jnp.bfloat16


def _multiplicity(dist):
    dist = np.asarray(dist, np.int64)
    mult = np.zeros(dist.shape, np.int64)
    for window, dil in DIL_PAIRS:
        mult += (dist >= 0) & (dist % dil == 0) & (dist // dil <= window // dil)
    return mult


def _log_mult_bias(dist, valid):
    mult = _multiplicity(dist) * np.asarray(valid, np.int64)
    return np.where(mult > 0, np.log(np.maximum(mult, 1)), NEG_INF).astype(np.float32)


def _rope_tables(pos):
    half = ROT_DIM // 2
    inv_freq = jnp.power(ROPE_THETA, -jnp.arange(half, dtype=F32) * 2.0 / ROT_DIM)
    ang = pos.astype(F32)[:, None] * inv_freq[None, :]
    cos, sin = jnp.cos(ang), jnp.sin(ang)
    n = pos.shape[0]
    rest = HEAD_DIM - ROT_DIM
    c = jnp.concatenate([cos, cos, jnp.ones((n, rest), F32)], axis=1)
    s1 = jnp.concatenate([jnp.zeros((n, half), F32), sin, jnp.zeros((n, rest), F32)], axis=1)
    s2 = jnp.concatenate([-sin, jnp.zeros((n, half + rest), F32)], axis=1)
    return tuple(jnp.tile(t, (1, N_HEADS)) for t in (c, s1, s2))


def _params(semantics):
    return pltpu.CompilerParams(dimension_semantics=semantics, vmem_limit_bytes=V7X_VMEM_LIMIT_BYTES)


def _resident(shape):
    zeros = (0,) * len(shape)
    return pl.BlockSpec(shape, lambda *_: zeros, pipeline_mode=pl.Buffered(1))


def _ada_kernel(c_ref, w_ref, b_ref, o_ref):
    c = c_ref[...]
    sc = (c * jax.nn.sigmoid(c)).astype(BF16)
    o_ref[0] = jnp.dot(sc, w_ref[0].astype(BF16), preferred_element_type=F32) + b_ref[0]


def _ada(c_all, w_ada, b_ada):
    depth, _, n_out = w_ada.shape
    n = c_all.shape[0]
    tn = 1536
    return pl.pallas_call(
        _ada_kernel,
        grid=(depth, n_out // tn),
        in_specs=[
            pl.BlockSpec((n, D_MODEL), lambda i, j: (0, 0)),
            pl.BlockSpec((1, D_MODEL, tn), lambda i, j: (i, 0, j)),
            pl.BlockSpec((1, 1, tn), lambda i, j: (i, 0, j)),
        ],
        out_specs=pl.BlockSpec((1, n, tn), lambda i, j: (i, 0, j)),
        out_shape=jax.ShapeDtypeStruct((depth, n, n_out), F32),
        compiler_params=_params(("arbitrary", "arbitrary")),
        name="ada_mod",
    )(c_all, w_ada, b_ada.reshape(depth, 1, n_out))


def _in_kernel(x_ref, mod_ref, w_ref, c_ref, s1_ref, s2_ref, *out_refs, prompt):
    nb, rows, _ = x_ref.shape
    m = nb * rows
    h = x_ref[...] * (1.0 + mod_ref[:, 1:2, :]) + mod_ref[:, 0:1, :]
    z = jnp.dot(h.reshape(m, D_MODEL).astype(BF16), w_ref[...], preferred_element_type=F32)
    c, s1, s2 = c_ref[...][None], s1_ref[...][None], s2_ref[...][None]

    def rope(t):
        lo = pltpu.roll(t, ROT_DIM // 2, 1).reshape(nb, rows, D_ATTN)
        hi = pltpu.roll(t, D_ATTN - ROT_DIM // 2, 1).reshape(nb, rows, D_ATTN)
        return t.reshape(nb, rows, D_ATTN) * c + lo * s1 + hi * s2

    q = rope(z[:, :D_ATTN]) * (HEAD_DIM ** -0.5)
    k = rope(z[:, D_ATTN:2 * D_ATTN])
    v = z[:, 2 * D_ATTN:3 * D_ATTN].reshape(nb, rows, D_ATTN)
    ga = z[:, 3 * D_ATTN:3 * D_ATTN + D_CONV]
    gb = z[:, 3 * D_ATTN + D_CONV:]
    u = (ga * jax.nn.sigmoid(gb)).reshape(nb, rows, D_CONV)
    if prompt:
        k_ref, v_ref, u_ref, qa_ref, ka_ref, va_ref = out_refs
        for hp in range(HEAD_PAIRS):
            cols = slice(hp * V7X_LANES, (hp + 1) * V7X_LANES)
            qa_ref[0, hp] = q[0, :, cols].astype(BF16)
            ka_ref[0, hp] = k[0, :, cols].astype(BF16)
            va_ref[0, hp] = v[0, :, cols].astype(BF16)
    else:
        k_ref, v_ref, u_ref, q_ref = out_refs
        q_ref[...] = q
    k_ref[...] = k
    v_ref[...] = v
    u_ref[...] = u


def _in_proj(x, mod, w_in_b, tables, *, prompt, nb, rows):
    n_batch, seq, _ = x.shape
    grid = (n_batch // nb, seq // rows)
    tok = lambda width: pl.BlockSpec((nb, rows, width), lambda b, i: (b, i, 0))
    tab = pl.BlockSpec((rows, D_ATTN), lambda b, i: (i, 0))
    f32_out = jax.ShapeDtypeStruct((n_batch, seq, D_ATTN), F32)
    out_shape = [f32_out, f32_out, f32_out]
    out_specs = [tok(D_ATTN), tok(D_ATTN), tok(D_CONV)]
    if prompt:
        att = jax.ShapeDtypeStruct((n_batch, HEAD_PAIRS, seq, V7X_LANES), BF16)
        att_spec = pl.BlockSpec((1, HEAD_PAIRS, rows, V7X_LANES), lambda b, i: (b, 0, i, 0))
        out_shape += [att, att, att]
        out_specs += [att_spec, att_spec, att_spec]
    else:
        out_shape += [f32_out]
        out_specs += [tok(D_ATTN)]
    return pl.pallas_call(
        functools.partial(_in_kernel, prompt=prompt),
        grid=grid,
        in_specs=[
            tok(D_MODEL),
            pl.BlockSpec((nb, 6, D_MODEL), lambda b, i: (b, 0, 0)),
            _resident((D_MODEL, D_IN)),
            tab, tab, tab,
        ],
        out_specs=out_specs,
        out_shape=out_shape,
        compiler_params=_params(("arbitrary", "arbitrary")),
        name="in_proj_prompt" if prompt else "in_proj_sample",
    )(x, mod, w_in_b, *tables)


def _prompt_bias():
    n_blk = WINDOW_MAX // ATT_BLOCK
    r = np.arange(ATT_BLOCK)
    dist = (np.arange(n_blk)[:, None, None] * ATT_BLOCK + r[None, :, None] - r[None, None, :])
    return _log_mult_bias(dist, np.ones_like(dist))


def _attn_prompt_kernel(q_ref, k_ref, v_ref, bias_ref, o_ref):
    i = pl.program_id(2)
    q = q_ref[0, 0]
    lane = lax.broadcasted_iota(jnp.int32, (1, V7X_LANES), 1)
    low = lane < HEAD_DIM
    zero = jnp.zeros_like(q)
    qq = jnp.concatenate([jnp.where(low, q, zero), jnp.where(low, zero, q)], axis=0)

    def step(kb, carry):
        m, l, acc = carry
        start = pl.multiple_of(kb * ATT_BLOCK, ATT_BLOCK)
        kblk = k_ref[0, 0, pl.ds(start, ATT_BLOCK), :]
        vblk = v_ref[0, 0, pl.ds(start, ATT_BLOCK), :]
        s = lax.dot_general(qq, kblk, (((1,), (1,)), ((), ())), preferred_element_type=F32)
        bias = bias_ref[i - kb]
        s = s + jnp.concatenate([bias, bias], axis=0)
        m_new = jnp.maximum(m, jnp.max(s, axis=1, keepdims=True))
        p = jnp.exp(s - m_new)
        alpha = jnp.exp(m - m_new)
        l = alpha * l + jnp.sum(p, axis=1, keepdims=True)
        acc = alpha * acc + jnp.dot(p.astype(BF16), vblk, preferred_element_type=F32)
        return m_new, l, acc

    init = (jnp.full((2 * ATT_BLOCK, 1), NEG_INF, F32), jnp.zeros((2 * ATT_BLOCK, 1), F32),
            jnp.zeros((2 * ATT_BLOCK, V7X_LANES), F32))
    _, l, acc = lax.fori_loop(0, i + 1, step, init)
    o = acc / l
    o_ref[0] = jnp.where(low, o[:ATT_BLOCK], o[ATT_BLOCK:])


def _attn_prompt(qa, ka, va, bias):
    n_batch, _, seq, _ = qa.shape
    n_blk = seq // ATT_BLOCK
    kv_spec = pl.BlockSpec((1, 1, seq, V7X_LANES), lambda b, hp, i: (b, hp, 0, 0))
    return pl.pallas_call(
        _attn_prompt_kernel,
        grid=(n_batch, HEAD_PAIRS, n_blk),
        in_specs=[
            pl.BlockSpec((1, 1, ATT_BLOCK, V7X_LANES), lambda b, hp, i: (b, hp, i, 0)),
            kv_spec, kv_spec,
            _resident(bias.shape),
        ],
        out_specs=pl.BlockSpec((1, ATT_BLOCK, V7X_LANES), lambda b, hp, i: (b, i, hp)),
        out_shape=jax.ShapeDtypeStruct((n_batch, seq, D_ATTN), F32),
        compiler_params=_params(("arbitrary", "arbitrary", "arbitrary")),
        name="attn_prompt",
    )(qa, ka, va, bias)


def _sample_bias(buf, t_new):
    t = np.arange(t_new)[:, None]
    pos_c = np.arange(buf)[None, :]
    dist_c = buf + t - pos_c
    bias_c = _log_mult_bias(dist_c, pos_c >= 0)
    tn = np.arange(V7X_LANES)[None, :]
    bias_n = _log_mult_bias(t - tn, tn < t_new)
    return np.tile(bias_c, (N_HEADS, 1)), np.tile(bias_n, (N_HEADS, 1))


def _attn_sample_kernel(q_ref, kn_ref, vn_ref, kc_ref, vc_ref, bc_ref, bn_ref, o_ref):
    t_new = q_ref.shape[1]
    q = q_ref[0]
    head = lax.broadcasted_iota(jnp.int32, (1, D_ATTN), 1) // HEAD_DIM
    zero = jnp.zeros_like(q)
    qm = jnp.concatenate([jnp.where(head == h, q, zero) for h in range(N_HEADS)], axis=0).astype(BF16)
    pad = jnp.zeros((V7X_LANES - t_new, D_ATTN), F32)
    kn = jnp.concatenate([kn_ref[0], pad], axis=0).astype(BF16)
    vn = jnp.concatenate([vn_ref[0], pad], axis=0).astype(BF16)
    kc_t = kc_ref[0, 0].astype(BF16)
    vc_t = vc_ref[0, 0].astype(BF16)
    nt = (((1,), (1,)), ((), ()))
    s_c = jnp.dot(qm, kc_t, preferred_element_type=F32) + bc_ref[...]
    s_n = lax.dot_general(qm, kn, nt, preferred_element_type=F32) + bn_ref[...]
    m = jnp.maximum(jnp.max(s_c, axis=1, keepdims=True), jnp.max(s_n, axis=1, keepdims=True))
    p_c = jnp.exp(s_c - m)
    p_n = jnp.exp(s_n - m)
    l = jnp.sum(p_c, axis=1, keepdims=True) + jnp.sum(p_n, axis=1, keepdims=True)
    o_full = (lax.dot_general(p_c.astype(BF16), vc_t, nt, preferred_element_type=F32)
              + jnp.dot(p_n.astype(BF16), vn, preferred_element_type=F32)) / l
    out = jnp.zeros((t_new, D_ATTN), F32)
    for h in range(N_HEADS):
        out = jnp.where(head == h, o_full[h * t_new:(h + 1) * t_new], out)
    o_ref[0] = out


def _attn_sample(q, k_new, v_new, cache_k, cache_v, layer, bias_c, bias_n):
    n_batch, t_new, _ = q.shape
    buf = cache_k.shape[3]
    new_spec = pl.BlockSpec((1, t_new, D_ATTN), lambda b: (b, 0, 0))
    cache_spec = pl.BlockSpec((1, 1, D_ATTN, buf), lambda b: (layer, b, 0, 0))
    return pl.pallas_call(
        _attn_sample_kernel,
        grid=(n_batch,),
        in_specs=[new_spec, new_spec, new_spec, cache_spec, cache_spec,
                  _resident(bias_c.shape), _resident(bias_n.shape)],
        out_specs=new_spec,
        out_shape=jax.ShapeDtypeStruct((n_batch, t_new, D_ATTN), F32),
        compiler_params=_params(("arbitrary",)),
        name="attn_sample",
    )(q, k_new, v_new, cache_k, cache_v, bias_c, bias_n)


def _layernorm(x, g, b):
    mu = jnp.mean(x, axis=-1, keepdims=True)
    xc = x - mu
    var = jnp.mean(xc * xc, axis=-1, keepdims=True)
    return xc * lax.rsqrt(var + LN_EPS) * g + b


def _rmsnorm(x, g):
    return x * lax.rsqrt(jnp.mean(x * x, axis=-1, keepdims=True) + LN_EPS) * g


def _out_kernel(x_ref, mod_ref, o_ref, u_ref, hist_ref, cw_ref, cb_ref, clg_ref, clb_ref, ba_ref, bc_ref,
                wo_ref, l1g_ref, l1b_ref, wg_ref, wu_ref, wd_ref, l2g_ref, l2b_ref, y_ref, uh_ref,
                *, prompt, alpha):
    nb, rows, _ = x_ref.shape
    m = nb * rows
    hist = hist_ref[...]
    if prompt:
        hist = jnp.where(pl.program_id(1) > 0, hist, jnp.zeros_like(hist))
    uh_ref[:, :HIST_ROWS, :] = hist
    uh_ref[:, HIST_ROWS:, :] = u_ref[...]
    conv = jnp.broadcast_to(cb_ref[...][None], (nb, rows, D_CONV))
    for j in range(CONV_WIDTH):
        conv = conv + cw_ref[j:j + 1, :][None] * uh_ref[:, pl.ds(HIST_PAD + j, rows), :]
    cz = _layernorm(conv.reshape(m, D_CONV), clg_ref[...], clb_ref[...])
    cz = cz * jax.nn.sigmoid(cz)
    oa = _rmsnorm(o_ref[...].reshape(m, D_ATTN), ba_ref[...])
    oc = _rmsnorm(cz, bc_ref[...])
    merged = jnp.concatenate([oa, oc], axis=1).astype(BF16)
    mix = jnp.dot(merged, wo_ref[...], preferred_element_type=F32).reshape(nb, rows, D_MODEL)
    x = x_ref[...]
    x1 = _layernorm(alpha * x + mod_ref[:, 2:3, :] * mix, l1g_ref[...][None], l1b_ref[...][None])
    h = (x1 * (1.0 + mod_ref[:, 4:5, :]) + mod_ref[:, 3:4, :]).reshape(m, D_MODEL).astype(BF16)
    f = jnp.zeros((m, D_MODEL), F32)
    for c in range(D_FF // FFN_CHUNK):
        cols = slice(c * FFN_CHUNK, (c + 1) * FFN_CHUNK)
        gate = jnp.dot(h, wg_ref[:, cols], preferred_element_type=F32)
        up = jnp.dot(h, wu_ref[:, cols], preferred_element_type=F32)
        act = (gate * jax.nn.sigmoid(gate) * up).astype(BF16)
        f = f + jnp.dot(act, wd_ref[cols, :], preferred_element_type=F32)
    f = f.reshape(nb, rows, D_MODEL)
    y_ref[...] = _layernorm(alpha * x1 + mod_ref[:, 5:6, :] * f, l2g_ref[...][None], l2b_ref[...][None])


def _out_proj(x, mod, o, u, hist_src, vecs, mats, *, prompt, nb, rows, alpha):
    n_batch, seq, _ = x.shape
    grid = (n_batch // nb, seq // rows)
    tok = lambda width: pl.BlockSpec((nb, rows, width), lambda b, i: (b, i, 0))
    if prompt:
        per_tile = rows // HIST_ROWS
        hist_spec = pl.BlockSpec((1, HIST_ROWS, D_CONV), lambda b, i: (b, jnp.maximum(i * per_tile - 1, 0), 0))
    else:
        hist_spec = pl.BlockSpec((nb, HIST_ROWS, D_CONV), lambda b, i: (b, 0, 0))
    cw, cb, clg, clb, ba, bc, l1g, l1b, l2g, l2b = vecs
    wo, wg, wu, wd = mats
    operands = [x, mod, o, u, hist_src, cw, cb, clg, clb, ba, bc, wo, l1g, l1b, wg, wu, wd, l2g, l2b]
    in_specs = [tok(D_MODEL), pl.BlockSpec((nb, 6, D_MODEL), lambda b, i: (b, 0, 0)),
                tok(D_ATTN), tok(D_CONV), hist_spec]
    in_specs += [_resident(a.shape) for a in operands[5:]]
    return pl.pallas_call(
        functools.partial(_out_kernel, prompt=prompt, alpha=alpha),
        grid=grid,
        in_specs=in_specs,
        out_specs=tok(D_MODEL),
        out_shape=jax.ShapeDtypeStruct((n_batch, seq, D_MODEL), F32),
        scratch_shapes=[pltpu.VMEM((nb, HIST_ROWS + rows, D_CONV), F32)],
        compiler_params=_params(("arbitrary", "arbitrary")),
        name="out_proj_prompt" if prompt else "out_proj_sample",
    )(*operands)


def kernel(x_prompt, x_sample, cache_k, cache_v, state_conv, c_prompt, c_sample, w_in, conv_w, conv_b, conv_ln_g, conv_ln_b, beta_attn, beta_conv, w_out, ln1_g, ln1_b, w_gate, w_up, w_down, ln2_g, ln2_b, w_ada, b_ada):
    depth = w_in.shape[0]
    n_p, seq, _ = x_prompt.shape
    n_s, t_new, _ = x_sample.shape
    buf = cache_k.shape[2]
    alpha = (2 * depth) ** 0.25

    mod = _ada(jnp.concatenate([c_prompt, c_sample], axis=0), w_ada, b_ada)
    mod = mod.reshape(depth, n_p + n_s, 6, D_MODEL)
    tables_p = _rope_tables(jnp.arange(seq))
    tables_s = _rope_tables(PAST_LEN + jnp.arange(t_new))
    bias_p = jnp.asarray(_prompt_bias())
    bias_c, bias_n = (jnp.asarray(b) for b in _sample_bias(buf, t_new))
    cache_k = jnp.transpose(cache_k, (0, 1, 3, 4, 2)).reshape(depth, n_s, D_ATTN, buf)
    cache_v = jnp.transpose(cache_v, (0, 1, 3, 4, 2)).reshape(depth, n_s, D_ATTN, buf)
    hist_s = jnp.pad(state_conv, ((0, 0), (0, 0), (HIST_PAD, 0), (0, 0)))
    row = lambda a: a.reshape(1, -1)

    xp, xs = x_prompt, x_sample
    outs = {name: [] for name in ("kp", "vp", "cp", "ks", "vs", "cs")}
    for i in range(depth):
        w_in_b = w_in[i].astype(BF16)
        mats = tuple(w[i].astype(BF16) for w in (w_out, w_gate, w_up, w_down))
        vecs = (conv_w[i],) + tuple(row(a[i]) for a in (conv_b, conv_ln_g, conv_ln_b, beta_attn, beta_conv,
                                                        ln1_g, ln1_b, ln2_g, ln2_b))
        mod_p, mod_s = mod[i, :n_p], mod[i, n_p:]

        k_p, v_p, u_p, qa, ka, va = _in_proj(xp, mod_p, w_in_b, tables_p, prompt=True, nb=1, rows=512)
        o_p = _attn_prompt(qa, ka, va, bias_p)
        xp = _out_proj(xp, mod_p, o_p, u_p, u_p, vecs, mats, prompt=True, nb=1, rows=256, alpha=alpha)

        k_s, v_s, u_s, q_s = _in_proj(xs, mod_s, w_in_b, tables_s, prompt=False, nb=32, rows=t_new)
        o_s = _attn_sample(q_s, k_s, v_s, cache_k, cache_v, i, bias_c, bias_n)
        xs = _out_proj(xs, mod_s, o_s, u_s, hist_s[i], vecs, mats, prompt=False, nb=32, rows=t_new, alpha=alpha)

        keep = min(WINDOW_MAX, seq)
        outs["kp"].append(k_p[:, seq - keep:].reshape(n_p, keep, N_HEADS, HEAD_DIM))
        outs["vp"].append(v_p[:, seq - keep:].reshape(n_p, keep, N_HEADS, HEAD_DIM))
        outs["cp"].append(u_p[:, seq - CONV_HIST:])
        outs["ks"].append(k_s.reshape(n_s, t_new, N_HEADS, HEAD_DIM))
        outs["vs"].append(v_s.reshape(n_s, t_new, N_HEADS, HEAD_DIM))
        outs["cs"].append(jnp.concatenate([state_conv[i], u_s], axis=1)[:, -CONV_HIST:])
    st = {name: jnp.stack(vals) for name, vals in outs.items()}
    return (xp, xs, st["kp"], st["vp"], st["cp"], st["ks"], st["vs"], st["cs"])
```
